```python
import math
import jax, jax.numpy as jnp
from jax import lax
import numpy as np

D_MODEL = 2048
BATCH = 16
SEQ = 256
DEPTH = 4
DEC_BATCH = 4
DEC_SEQ = 4096
PAST_LEN = 256

GRID_W = 64
D_A = D_MODEL // 2
P_A = 64
H_A = D_A // P_A
G_A = 4
N_A = 128
CONV_K = 5
SSD_CHUNK = 128
XBC_W = D_A + 2 * G_A * N_A
D_B = D_MODEL // 2
N_B = 64
H_B = D_B // N_B
R_W = 64
R_A = 64
RWKV_GN_EPS = 64e-5
D_C = D_MODEL // 2
C_GRP = 16
G_C = D_C // C_GRP
N_C = 64
N_BRANCH = 3
RMS_EPS = 1e-6
SPLIT_SIZES = (XBC_W, D_A, H_A, D_B, D_B, D_B, D_B, R_W, R_A, D_C, D_C, N_BRANCH * D_MODEL)
SPLIT_IDX = tuple(sum(SPLIT_SIZES[:i + 1]) for i in range(len(SPLIT_SIZES) - 1))
D_IN = sum(SPLIT_SIZES)

kernel_name = 'hybrid_ssd_rwkv7_s5_diffusion_step'


def rms_norm(x, g):
    xf = x.astype(jnp.float32)
    y = xf * lax.rsqrt(jnp.mean(xf * xf, axis=-1, keepdims=True) + RMS_EPS)
    return (y * g.astype(jnp.float32)).astype(x.dtype)


def flip(t):
    return jnp.flip(t, axis=1)


def line_conv(x, w, b, line_len):
    bsz, L, ch = x.shape
    xl = x.reshape(bsz * (L // line_len), line_len, ch)
    y = lax.conv_general_dilated(xl, w.astype(x.dtype)[:, None, :], window_strides=(1,),
                                 padding=[(CONV_K // 2, CONV_K // 2)],
                                 dimension_numbers=('NWC', 'WIO', 'NWC'), feature_group_count=ch)
    return y.reshape(bsz, L, ch) + b


def ssd_scan(x, dt, A, B, C, h0):
    f32 = jnp.float32
    bsz, L, H, P = x.shape
    G, N = B.shape[2], B.shape[3]
    J = H // G
    Q = SSD_CHUNK
    nc = L // Q
    xs = (x.astype(f32) * dt[..., None]).reshape(bsz, nc, Q, G, J, P)
    Bc = B.astype(f32).reshape(bsz, nc, Q, G, N)
    Cc = C.astype(f32).reshape(bsz, nc, Q, G, N)
    a = (dt * A).reshape(bsz, nc, Q, G, J).transpose(0, 3, 4, 1, 2)
    acs = jnp.cumsum(a, axis=-1)
    causal = jnp.tril(jnp.ones((Q, Q), dtype=bool))
    Lmat = jnp.exp(jnp.where(causal, acs[..., :, None] - acs[..., None, :], -jnp.inf))
    y_diag = jnp.einsum('bcqgn,bcsgn,bgjcqs,bcsgjp->bcqgjp', Cc, Bc, Lmat, xs)
    decay_states = jnp.exp(acs[..., -1:] - acs)
    states = jnp.einsum('bcsgn,bgjcs,bcsgjp->bcgjpn', Bc, decay_states, xs)
    states = jnp.concatenate([h0.astype(f32).reshape(bsz, 1, G, J, P, N), states], axis=1)
    chunk_cs = jnp.cumsum(jnp.pad(acs[..., -1], ((0, 0), (0, 0), (0, 0), (1, 0))), axis=-1)
    cmask = jnp.tril(jnp.ones((nc + 1, nc + 1), dtype=bool))
    decay_chunk = jnp.exp(jnp.where(cmask, chunk_cs[..., :, None] - chunk_cs[..., None, :], -jnp.inf))
    new_states = jnp.einsum('bgjzc,bcgjpn->bzgjpn', decay_chunk, states)
    y_off = jnp.einsum('bcqgn,bcgjpn,bgjcq->bcqgjp', Cc, new_states[:, :-1], jnp.exp(acs))
    y = (y_diag + y_off).reshape(bsz, L, H, P)
    return y, new_states[:, -1].reshape(bsz, H, P, N)


def ssd_branch(xbc, za, dta, s0, line_len, p):
    f32 = jnp.float32
    bsz, L, _ = xbc.shape
    xbc = jax.nn.silu(line_conv(xbc, p['conv_w'], p['conv_b'], line_len))
    xa, Ba, Ca = jnp.split(xbc, (D_A, D_A + G_A * N_A), axis=-1)
    xa = xa.reshape(bsz, L, H_A, P_A)
    Ba = Ba.reshape(bsz, L, G_A, N_A)
    Ca = Ca.reshape(bsz, L, G_A, N_A)
    y = p['ssd_d'].astype(f32)[:, None] * xa.astype(f32)
    finals = []
    for d in range(2):
        dt = jax.nn.softplus(dta.astype(f32) + p['ssd_dt_bias'][d].astype(f32))
        A = -jnp.exp(p['ssd_a_log'][d].astype(f32))
        if d == 0:
            yd, sd = ssd_scan(xa, dt, A, Ba, Ca, s0[:, d])
        else:
            yd, sd = ssd_scan(flip(xa), flip(dt), A, flip(Ba), flip(Ca), s0[:, d])
            yd = flip(yd)
        y = y + yd
        finals.append(sd)
    y = y.reshape(bsz, L, D_A) * jax.nn.silu(za.astype(f32))
    y = rms_norm(y, p['ssd_norm'])
    return y.astype(xbc.dtype), jnp.stack(finals, axis=1)


def rwkv_scan(r, w, k, v, kk, a, s0, reverse):
    def step(S, inp):
        r_t, w_t, k_t, v_t, kk_t, a_t = inp
        s_kk = jnp.einsum('bhvk,bhk->bhv', S, kk_t)
        S = (S * w_t[:, :, None, :] - s_kk[..., None] * (kk_t * a_t)[:, :, None, :]
             + v_t[..., None] * k_t[:, :, None, :])
        return S, jnp.einsum('bhvk,bhk->bhv', S, r_t)
    xs = tuple(jnp.swapaxes(t, 0, 1) for t in (r, w, k, v, kk, a))
    S, ys = lax.scan(step, s0.astype(jnp.float32), xs, reverse=reverse)
    return jnp.swapaxes(ys, 0, 1), S


def rwkv_branch(r, k, v, gpath, lw, la, s0, p):
    f32 = jnp.float32
    bsz, L, _ = r.shape
    hd = lambda t: t.astype(f32).reshape(bsz, L, H_B, N_B)
    r = r.astype(f32)
    k = k.astype(f32)
    v = v.astype(f32)
    kk = hd(k * p['rw_kk'].astype(f32))
    kk = kk * lax.rsqrt(jnp.sum(kk * kk, axis=-1, keepdims=True) + 1e-12)
    tw = jnp.tanh(lw.astype(f32))
    la = la.astype(f32)
    rh, vh = hd(r), hd(v)
    y = 0.0
    bonus = 0.0
    finals = []
    for d in range(2):
        w_log = -jax.nn.softplus(-(p['rw_w0'][d] + tw @ p['rw_w2'][d])) - 0.5
        w = jnp.exp(-jnp.exp(w_log.astype(f32)))
        a = jax.nn.sigmoid((p['rw_a0'][d] + la @ p['rw_a2'][d]).astype(f32))
        kd = k * (1.0 + (a - 1.0) * p['rw_ka'].astype(f32))
        yd, sd = rwkv_scan(rh, hd(w), hd(kd), vh, kk, hd(a), s0[:, d], d == 1)
        y = y + yd
        bonus = bonus + jnp.sum(rh * hd(kd) * p['rw_rk'].astype(f32), axis=-1, keepdims=True) * vh
        finals.append(sd)
    mu = jnp.mean(y, axis=-1, keepdims=True)
    var = jnp.mean(jnp.square(y - mu), axis=-1, keepdims=True)
    y = ((y - mu) * lax.rsqrt(var + RWKV_GN_EPS)).reshape(bsz, L, D_B)
    y = y * p['rw_ln_w'] + p['rw_ln_b'] + bonus.reshape(bsz, L, D_B)
    y = y * jax.nn.silu(gpath.astype(f32))
    return y.astype(gpath.dtype), jnp.stack(finals, axis=1)


def complex_affine_combine(e1, e2):
    a1r, a1i, b1r, b1i = e1
    a2r, a2i, b2r, b2i = e2
    return (a2r * a1r - a2i * a1i, a2r * a1i + a2i * a1r,
            a2r * b1r - a2i * b1i + b2r, a2r * b1i + a2i * b1r + b2i)


def s5_branch(uc, gc, s0_re, s0_im, p):
    f32 = jnp.float32
    bsz, L, _ = uc.shape
    u = uc.astype(f32).reshape(bsz, L, G_C, C_GRP)
    y = 0.0
    fin_re, fin_im = [], []
    for d in range(2):
        lam_re = p['s5_a_re'][d].astype(f32)
        lam_im = p['s5_a_im'][d].astype(f32)
        step = jnp.exp(p['s5_log_dt'][d].astype(f32))[:, None]
        mag = jnp.exp(lam_re * step)
        ab_re = mag * jnp.cos(lam_im * step)
        ab_im = mag * jnp.sin(lam_im * step)
        den = lam_re * lam_re + lam_im * lam_im
        f_re = ((ab_re - 1.0) * lam_re + ab_im * lam_im) / den
        f_im = (ab_im * lam_re - (ab_re - 1.0) * lam_im) / den
        bu_re = jnp.einsum('blgi,gni->blgn', u, p['s5_b_re'][d].astype(f32))
        bu_im = jnp.einsum('blgi,gni->blgn', u, p['s5_b_im'][d].astype(f32))
        b_r = f_re * bu_re - f_im * bu_im
        b_i = f_re * bu_im + f_im * bu_re
        h0r = s0_re[:, d].astype(f32)
        h0i = s0_im[:, d].astype(f32)
        edge = 0 if d == 0 else L - 1
        b_r = b_r.at[:, edge].add(ab_re * h0r - ab_im * h0i)
        b_i = b_i.at[:, edge].add(ab_re * h0i + ab_im * h0r)
        a_r = jnp.broadcast_to(ab_re, b_r.shape)
        a_i = jnp.broadcast_to(ab_im, b_r.shape)
        _, _, h_re, h_im = lax.associative_scan(complex_affine_combine, (a_r, a_i, b_r, b_i),
                                                reverse=(d == 1), axis=1)
        y = (y + jnp.einsum('blgn,gin->blgi', h_re, p['s5_c_re'][d].astype(f32))
             - jnp.einsum('blgn,gin->blgi', h_im, p['s5_c_im'][d].astype(f32)))
        fin_re.append(h_re[:, L - 1 - edge])
        fin_im.append(h_im[:, L - 1 - edge])
    y = y.reshape(bsz, L, D_C) + p['s5_d'].astype(f32) * uc.astype(f32)
    z = jax.nn.gelu(y) @ p['s5_glu_w'] + p['s5_glu_b']
    y = z[..., :D_C] * jax.nn.sigmoid(z[..., D_C:])
    y = y * jax.nn.silu(gc.astype(f32))
    return y.astype(uc.dtype), jnp.stack(fin_re, axis=1), jnp.stack(fin_im, axis=1)


def mixer(u, st0, line_len, p):
    s_ssd, s_rwkv, s_re, s_im = st0
    proj = u @ p['w_in']
    xbc, za, dta, r, k, v, gb, lw, la, uc, gc, gl = jnp.split(proj, SPLIT_IDX, axis=-1)
    ya, fa = ssd_branch(xbc, za, dta, s_ssd, line_len, p)
    yb, fb = rwkv_branch(r, k, v, gb, lw, la, s_rwkv, p)
    yc, fre, fim = s5_branch(uc, gc, s_re, s_im, p)
    gates = jax.nn.sigmoid(gl.astype(jnp.float32)).reshape(gl.shape[0], gl.shape[1], N_BRANCH, D_MODEL)
    wb = p['w_branch']
    m = (gates[:, :, 0] * (ya @ wb[0]) + gates[:, :, 1] * (yb @ wb[1])
         + gates[:, :, 2] * (yc @ wb[2]))
    y = m.astype(u.dtype) @ p['w_mo']
    return y, (fa, fb, fre, fim)


def apply_layer(h, cond, st0, rows, transposed, p):
    bsz, L, _ = h.shape
    mod = jax.nn.silu(cond) @ p['w_ada'] + p['b_ada']
    shift, scale, gate = jnp.split(mod[:, None, :], 3, axis=-1)
    u = rms_norm(h, p['g_pre']) * (1.0 + scale) + shift
    if rows is None:
        line_len = L
    elif transposed:
        u = u.reshape(bsz, rows, GRID_W, D_MODEL).transpose(0, 2, 1, 3).reshape(bsz, L, D_MODEL)
        line_len = rows
    else:
        line_len = GRID_W
    y, st = mixer(u, st0, line_len, p)
    if rows is not None and transposed:
        y = y.reshape(bsz, GRID_W, rows, D_MODEL).transpose(0, 2, 1, 3).reshape(bsz, L, D_MODEL)
    return h + gate * rms_norm(y, p['g_post']), st


def setup_inputs(seed: int = 0) -> dict:
    key = jax.random.key(seed)
    keys = list(jax.random.split(key, 48))
    f32 = jnp.float32

    def nrm(shape, s):
        return jax.random.normal(keys.pop(), shape, f32) * s

    def unif(shape, lo, hi):
        return jax.random.uniform(keys.pop(), shape, f32, lo, hi)

    D = D_MODEL
    dt_init = jnp.exp(unif((DEPTH, 2, H_A), math.log(1e-3), math.log(1e-1)))
    return {
        'x_prompt': nrm((BATCH, SEQ, D), 1.0),
        'x_sample': nrm((DEC_BATCH, DEC_SEQ, D), 1.0),
        'c': nrm((DEC_BATCH, D), 1.0),
        'state_ssd': nrm((DEC_BATCH, DEPTH, 2, H_A, P_A, N_A), 0.5),
        'state_rwkv': nrm((DEC_BATCH, DEPTH, 2, H_B, N_B, N_B), 0.5),
        'state_s5_re': nrm((DEC_BATCH, DEPTH, 2, G_C, N_C), 0.5),
        'state_s5_im': nrm((DEC_BATCH, DEPTH, 2, G_C, N_C), 0.5),
        'c_ctx': nrm((D,), 1.0),
        'w_ada': nrm((DEPTH, D, 3 * D), 0.2 * D ** -0.5),
        'b_ada': nrm((DEPTH, 3 * D), 0.02),
        'g_pre': 1.0 + nrm((DEPTH, D), 0.02),
        'g_post': 1.0 + nrm((DEPTH, D), 0.02),
        'w_in': nrm((DEPTH, D, D_IN), D ** -0.5),
        'conv_w': nrm((DEPTH, CONV_K, XBC_W), CONV_K ** -0.5),
        'conv_b': nrm((DEPTH, XBC_W), 0.02),
        'ssd_a_log': jnp.log(unif((DEPTH, 2, H_A), 1.0, 16.0)),
        'ssd_dt_bias': dt_init + jnp.log(-jnp.expm1(-dt_init)),
        'ssd_d': 1.0 + nrm((DEPTH, H_A), 0.1),
        'ssd_norm': 1.0 + nrm((DEPTH, D_A), 0.02),
        'rw_w0': unif((DEPTH, 2, D_B), -6.0, -1.0),
        'rw_w2': nrm((DEPTH, 2, R_W, D_B), 0.1 * R_W ** -0.5),
        'rw_a0': nrm((DEPTH, 2, D_B), 0.1),
        'rw_a2': nrm((DEPTH, 2, R_A, D_B), 0.1 * R_A ** -0.5),
        'rw_kk': 0.85 + nrm((DEPTH, D_B), 0.02),
        'rw_ka': 1.0 + nrm((DEPTH, D_B), 0.02),
        'rw_rk': nrm((DEPTH, H_B, N_B), 0.1),
        'rw_ln_w': 1.0 + nrm((DEPTH, D_B), 0.02),
        'rw_ln_b': nrm((DEPTH, D_B), 0.02),
        's5_a_re': -0.5 + nrm((DEPTH, 2, G_C, N_C), 0.01),
        's5_a_im': math.pi * jnp.arange(N_C, dtype=f32) + nrm((DEPTH, 2, G_C, N_C), 0.01),
        's5_log_dt': unif((DEPTH, 2, G_C), math.log(1e-3), math.log(1e-1)),
        's5_b_re': nrm((DEPTH, 2, G_C, N_C, C_GRP), (2 * C_GRP) ** -0.5),
        's5_b_im': nrm((DEPTH, 2, G_C, N_C, C_GRP), (2 * C_GRP) ** -0.5),
        's5_c_re': nrm((DEPTH, 2, G_C, C_GRP, N_C), (2 * N_C) ** -0.5),
        's5_c_im': nrm((DEPTH, 2, G_C, C_GRP, N_C), (2 * N_C) ** -0.5),
        's5_d': nrm((DEPTH, D_C), 1.0),
        's5_glu_w': nrm((DEPTH, D_C, 2 * D_C), D_C ** -0.5),
        's5_glu_b': nrm((DEPTH, 2 * D_C), 0.02),
        'w_branch': nrm((DEPTH, N_BRANCH, D_A, D), D_A ** -0.5),
        'w_mo': nrm((DEPTH, D, D), D ** -0.5),
    }


def reference(x_prompt, x_sample, c, state_ssd, state_rwkv, state_s5_re, state_s5_im, c_ctx,
              w_ada, b_ada, g_pre, g_post, w_in, conv_w, conv_b, ssd_a_log, ssd_dt_bias, ssd_d, ssd_norm,
              rw_w0, rw_w2, rw_a0, rw_a2, rw_kk, rw_ka, rw_rk, rw_ln_w, rw_ln_b,
              s5_a_re, s5_a_im, s5_log_dt, s5_b_re, s5_b_im, s5_c_re, s5_c_im, s5_d, s5_glu_w, s5_glu_b,
              w_branch, w_mo):
    f32 = jnp.float32

    def layer_params(l):
        return {
            'w_ada': w_ada[l], 'b_ada': b_ada[l], 'g_pre': g_pre[l], 'g_post': g_post[l],
            'w_in': w_in[l], 'conv_w': conv_w[l], 'conv_b': conv_b[l],
            'ssd_a_log': ssd_a_log[l], 'ssd_dt_bias': ssd_dt_bias[l], 'ssd_d': ssd_d[l], 'ssd_norm': ssd_norm[l],
            'rw_w0': rw_w0[l], 'rw_w2': rw_w2[l], 'rw_a0': rw_a0[l], 'rw_a2': rw_a2[l],
            'rw_kk': rw_kk[l], 'rw_ka': rw_ka[l], 'rw_rk': rw_rk[l], 'rw_ln_w': rw_ln_w[l], 'rw_ln_b': rw_ln_b[l],
            's5_a_re': s5_a_re[l], 's5_a_im': s5_a_im[l], 's5_log_dt': s5_log_dt[l],
            's5_b_re': s5_b_re[l], 's5_b_im': s5_b_im[l], 's5_c_re': s5_c_re[l], 's5_c_im': s5_c_im[l],
            's5_d': s5_d[l], 's5_glu_w': s5_glu_w[l], 's5_glu_b': s5_glu_b[l],
            'w_branch': w_branch[l], 'w_mo': w_mo[l],
        }

    bp = x_prompt.shape[0]
    zero_state = (jnp.zeros((bp, 2, H_A, P_A, N_A), f32), jnp.zeros((bp, 2, H_B, N_B, N_B), f32),
                  jnp.zeros((bp, 2, G_C, N_C), f32), jnp.zeros((bp, 2, G_C, N_C), f32))
    ctx_cond = c_ctx[None, :]
    h = x_prompt
    new_ssd, new_rwkv, new_re, new_im = [], [], [], []
    for l in range(DEPTH):
        h, (sa, sb, sre, sim) = apply_layer(h, ctx_cond, zero_state, None, False, layer_params(l))
        new_ssd.append(sa)
        new_rwkv.append(sb)
        new_re.append(sre)
        new_im.append(sim)

    rows = x_sample.shape[1] // GRID_W
    hs = x_sample
    for l in range(DEPTH):
        st0 = (state_ssd[:, l], state_rwkv[:, l], state_s5_re[:, l], state_s5_im[:, l])
        hs, _ = apply_layer(hs, c, st0, rows, l % 2 == 1, layer_params(l))

    return (h, hs, jnp.stack(new_ssd, axis=1), jnp.stack(new_rwkv, axis=1),
            jnp.stack(new_re, axis=1), jnp.stack(new_im, axis=1))
```

```python
import functools
import math

import jax
import jax.numpy as jnp
import numpy as np
from jax import lax
from jax.experimental import pallas as pl
from jax.experimental.pallas import tpu as pltpu

F32 = jnp.float32
BF16 = jnp.bfloat16

D_MODEL = 2048
DEPTH = 4
GRID_W = 64
D_A = 1024
P_A = 64
H_A = 16
G_A = 4
N_A = 128
CONV_K = 5
XBC_W = D_A + 2 * G_A * N_A
D_B = 1024
N_B = 64
H_B = 16
R_W = 64
R_A = 64
RWKV_GN_EPS = 64e-5
D_C = 1024
C_GRP = 16
G_C = 64
N_C = 64
N_BRANCH = 3
RMS_EPS = 1e-6

COL_XBC = 0
COL_GL = 2048
COL_ZA = 8192
COL_R = 9216
COL_K = 10240
COL_V = 11264
COL_GB = 12288
COL_UC = 13312
COL_GC = 14336
COL_DTA = 15360
COL_LWLA = 15488
N_PROJ = 15872

SSD_Q = 256
RWKV_Q = 128
S5_Q = 16
VMEM_LIMIT = 56 * 1024 * 1024


def _cparams():
    return pltpu.CompilerParams(vmem_limit_bytes=VMEM_LIMIT)


def _dot(a, b):
    return jnp.dot(a.astype(BF16), b.astype(BF16), preferred_element_type=F32)


def _dot_nt(a, b):
    return lax.dot_general(a.astype(BF16), b.astype(BF16), (((1,), (1,)), ((), ())),
                           preferred_element_type=F32)


def _split3(x):
    hi = x.astype(BF16)
    r1 = x - hi.astype(F32)
    mid = r1.astype(BF16)
    lo = (r1 - mid.astype(F32)).astype(BF16)
    return hi, mid, lo


def _dot_exact_lhs(a01, x):
    a = a01.astype(BF16)
    hi, mid, lo = _split3(x)
    f = lambda p: jnp.dot(a, p, preferred_element_type=F32)
    return f(hi) + f(mid) + f(lo)


def _dot_exact_rhs(x, b01):
    b = b01.astype(BF16)
    hi, mid, lo = _split3(x)
    f = lambda p: jnp.dot(p, b, preferred_element_type=F32)
    return f(hi) + f(mid) + f(lo)


def _sigmoid(x):
    return 1.0 / (1.0 + jnp.exp(-x))


def _silu(x):
    return x * _sigmoid(x)


def _softplus(x):
    return jnp.maximum(x, 0.0) + jnp.log(1.0 + jnp.exp(-jnp.abs(x)))


def _gelu_tanh(x):
    c = math.sqrt(2.0 / math.pi)
    return 0.5 * x * (1.0 + jnp.tanh(c * (x + 0.044715 * (x * x * x))))


def _adaln_kernel(c_ref, w_ref, b_ref, o_ref):
    s = _silu(c_ref[...])
    o_ref[0] = jnp.dot(s, w_ref[0], preferred_element_type=F32,
                       precision=lax.Precision.HIGHEST) + b_ref[0]


def _adaln(cond8, w_ada, b_ada):
    depth, d, n3 = w_ada.shape
    tn = 768
    return pl.pallas_call(
        _adaln_kernel,
        grid=(depth, n3 // tn),
        in_specs=[
            pl.BlockSpec((8, d), lambda l, j: (0, 0)),
            pl.BlockSpec((1, d, tn), lambda l, j: (l, 0, j)),
            pl.BlockSpec((1, 1, tn), lambda l, j: (l, 0, j)),
        ],
        out_specs=pl.BlockSpec((1, 8, tn), lambda l, j: (l, 0, j)),
        out_shape=jax.ShapeDtypeStruct((depth, 8, n3), F32),
        compiler_params=_cparams(),
        name="adaln",
    )(cond8, w_ada, b_ada.reshape(depth, 1, n3))


def _proj_kernel(h_ref, shift_ref, scale_ref, g_ref, w_ref, o_ref, u_scr):
    @pl.when(pl.program_id(1) == 0)
    def _():
        x = h_ref[...]
        ms = jnp.mean(x * x, axis=-1, keepdims=True)
        y = x * lax.rsqrt(ms + RMS_EPS) * g_ref[...]
        u_scr[...] = (y * (1.0 + scale_ref[0]) + shift_ref[0]).astype(BF16)

    o_ref[...] = jnp.dot(u_scr[...], w_ref[...], preferred_element_type=F32)


def _proj(h, modr, g_pre, w_p, row_base, blocks_per_row, tm):
    t, d = h.shape
    n = w_p.shape[1]
    tn = 512
    mod_idx = lambda which: (lambda i, j: ((row_base + i // blocks_per_row) * 3 + which, 0, 0))
    return pl.pallas_call(
        _proj_kernel,
        grid=(t // tm, n // tn),
        in_specs=[
            pl.BlockSpec((tm, d), lambda i, j: (i, 0)),
            pl.BlockSpec((1, 1, d), mod_idx(0)),
            pl.BlockSpec((1, 1, d), mod_idx(1)),
            pl.BlockSpec((1, d), lambda i, j: (0, 0)),
            pl.BlockSpec((d, tn), lambda i, j: (0, j)),
        ],
        out_specs=pl.BlockSpec((tm, tn), lambda i, j: (i, j)),
        out_shape=jax.ShapeDtypeStruct((t, n), F32),
        scratch_shapes=[pltpu.VMEM((tm, d), BF16)],
        compiler_params=_cparams(),
        name="proj",
    )(h, modr, modr, g_pre.reshape(1, d), w_p)


def _ssd_kernel(xbc_ref, dta_ref, dtat_ref, cw_ref, cb_ref, dtbr_ref, dtbc_ref, alr_ref, alc_ref,
                dsk_ref, h0_ref, y_ref, fin_ref, st_ref, *, q, line_len):
    d = pl.program_id(1)
    c = pl.program_id(2)
    nc = pl.num_programs(2)

    @pl.when(c == 0)
    def _():
        st_ref[...] = h0_ref[0, 0]

    x = xbc_ref[...]
    pos = lax.broadcasted_iota(jnp.int32, (q, 1), 0) % line_len
    acc = x * cw_ref[2:3, :] + cb_ref[...]
    for tap in (0, 1, 3, 4):
        sh = tap - 2
        shifted = pltpu.roll(x, (-sh) % q, 0)
        valid = jnp.logical_and(pos + sh >= 0, pos + sh < line_len)
        acc = acc + jnp.where(valid, shifted, 0.0) * cw_ref[tap:tap + 1, :]
    act = _silu(acc)
    xa = act[:, :D_A]
    bm = act[:, D_A:D_A + G_A * N_A]
    cm = act[:, D_A + G_A * N_A:]

    dt_col = _softplus(dta_ref[:, :H_A] + dtbr_ref[0])
    dt_row = _softplus(dtat_ref[...] + dtbc_ref[0])
    a_col = dt_col * (-jnp.exp(alr_ref[0]))
    a_row = dt_row * (-jnp.exp(alc_ref[0]))
    sgn = 1 - 2 * d
    row = lax.broadcasted_iota(jnp.int32, (q, q), 0)
    col = lax.broadcasted_iota(jnp.int32, (q, q), 1)
    diff = (row - col) * sgn
    incl = diff >= 0
    acs_col = _dot_exact_lhs(jnp.where(incl, 1.0, 0.0), a_col)
    acs_row = _dot_exact_rhs(a_row, jnp.where(diff <= 0, 1.0, 0.0))
    tot = jnp.sum(a_col, axis=0, keepdims=True)
    dec_end = jnp.exp(tot - acs_col)
    eacs = jnp.exp(acs_col)
    etot = jnp.exp(tot)
    fwd_f = jnp.where(d == 0, 1.0, 0.0)

    for g in range(G_A):
        bg = bm[:, g * N_A:(g + 1) * N_A]
        cg = cm[:, g * N_A:(g + 1) * N_A].astype(BF16)
        bgt = bg.T.astype(BF16)
        cb = jnp.dot(cg, bgt, preferred_element_type=F32)
        for j in range(H_A // G_A):
            h = g * (H_A // G_A) + j
            sl = slice(h * P_A, (h + 1) * P_A)
            xa_h = xa[:, sl]
            xs = xa_h * dt_col[:, h:h + 1]
            seg = acs_col[:, h:h + 1] - acs_row[h:h + 1, :]
            lmat = jnp.exp(jnp.where(incl, seg, -1e30))
            st_h = st_ref[:, sl]
            y = _dot(cb * lmat, xs) + _dot(cg, st_h) * eacs[:, h:h + 1]
            y_ref[0, :, sl] = y + fwd_f * dsk_ref[:, sl] * xa_h
            st_ref[:, sl] = st_h * etot[:, h:h + 1] + _dot(bgt, xs * dec_end[:, h:h + 1])

    @pl.when(c == nc - 1)
    def _():
        fin_ref[0, 0] = st_ref[...]


def _ssd(proj, dtat, conv_w, conv_b, dt_bias, a_log, d_skip, h0t, nseq, seq_len, line_len):
    q = SSD_Q
    nc = seq_len // q
    t = nseq * seq_len
    rowblk = lambda b, d, c: b * nc + c + d * (nc - 1 - 2 * c)
    kern = functools.partial(_ssd_kernel, q=q, line_len=line_len)
    return pl.pallas_call(
        kern,
        grid=(nseq, 2, nc),
        in_specs=[
            pl.BlockSpec((q, XBC_W), lambda b, d, c: (rowblk(b, d, c), COL_XBC // XBC_W)),
            pl.BlockSpec((q, 128), lambda b, d, c: (rowblk(b, d, c), COL_DTA // 128)),
            pl.BlockSpec((H_A, q), lambda b, d, c: (0, rowblk(b, d, c))),
            pl.BlockSpec((CONV_K, XBC_W), lambda b, d, c: (0, 0)),
            pl.BlockSpec((1, XBC_W), lambda b, d, c: (0, 0)),
            pl.BlockSpec((1, 1, H_A), lambda b, d, c: (d, 0, 0)),
            pl.BlockSpec((1, H_A, 1), lambda b, d, c: (d, 0, 0)),
            pl.BlockSpec((1, 1, H_A), lambda b, d, c: (d, 0, 0)),
            pl.BlockSpec((1, H_A, 1), lambda b, d, c: (d, 0, 0)),
            pl.BlockSpec((1, D_A), lambda b, d, c: (0, 0)),
            pl.BlockSpec((1, 1, N_A, D_A), lambda b, d, c: (b, d, 0, 0)),
        ],
        out_specs=[
            pl.BlockSpec((1, q, D_A), lambda b, d, c: (d, rowblk(b, d, c), 0)),
            pl.BlockSpec((1, 1, N_A, D_A), lambda b, d, c: (b, d, 0, 0)),
        ],
        out_shape=[
            jax.ShapeDtypeStruct((2, t, D_A), F32),
            jax.ShapeDtypeStruct((nseq, 2, N_A, D_A), F32),
        ],
        scratch_shapes=[pltpu.VMEM((N_A, D_A), F32)],
        compiler_params=_cparams(),
        name="ssd",
    )(proj, proj, dtat, conv_w, conv_b.reshape(1, XBC_W),
      dt_bias.reshape(2, 1, H_A), dt_bias.reshape(2, H_A, 1),
      a_log.reshape(2, 1, H_A), a_log.reshape(2, H_A, 1),
      jnp.repeat(d_skip, P_A).reshape(1, D_A), h0t)


def _rwkv_kernel(r_ref, k_ref, v_ref, lwla_ref, w0_ref, w2_ref, a0_ref, a2_ref, kkp_ref, ka_ref, rk_ref,
                 s0_ref, y_ref, bonus_ref, fin_ref, s_ref, *, q):
    d = pl.program_id(1)
    c = pl.program_id(3)
    nc = pl.num_programs(3)

    @pl.when(c == 0)
    def _():
        s_ref[...] = s0_ref[0, 0]

    r = r_ref[...]
    k = k_ref[...]
    v = v_ref[...]
    lw = lwla_ref[:, :R_W]
    la = lwla_ref[:, R_W:]
    wl = w0_ref[0] + _dot(jnp.tanh(lw), w2_ref[0])
    logw = -jnp.exp(-_softplus(-wl) - 0.5)
    a = _sigmoid(a0_ref[0] + _dot(la, a2_ref[0]))
    kd = k * (1.0 + (a - 1.0) * ka_ref[...])
    kkr = k * kkp_ref[...]

    sgn = 1 - 2 * d
    row = lax.broadcasted_iota(jnp.int32, (q, q), 0)
    col = lax.broadcasted_iota(jnp.int32, (q, q), 1)
    diff = (row - col) * sgn
    incl = diff >= 0
    strict = diff > 0
    eye = jnp.where(diff == 0, 1.0, 0.0)
    lc = _dot_exact_lhs(jnp.where(incl, 1.0, 0.0), logw)
    lce = lc - logw
    mid = lc[q // 2:q // 2 + 1, :]
    tot = jnp.sum(logw, axis=0, keepdims=True)
    g_in = jnp.exp(lc - mid)
    g_ex = jnp.exp(lce - mid)
    g_inv = jnp.exp(mid - lc)
    g_end = jnp.exp(tot - lc)
    g_abs = jnp.exp(lc)
    g_abse = jnp.exp(lce)
    etot = jnp.exp(tot)

    for j in range(2):
        sl = slice(j * N_B, (j + 1) * N_B)
        kk = kkr[:, sl]
        kk = kk * lax.rsqrt(jnp.sum(kk * kk, axis=-1, keepdims=True) + 1e-12)
        bb = kk * a[:, sl]
        r_h = r[:, sl]
        kd_h = kd[:, sl]
        v_h = v[:, sl]
        lhs = jnp.concatenate([kk * g_ex[:, sl], r_h * g_in[:, sl]], axis=0)
        rhs = jnp.concatenate([bb * g_inv[:, sl], kd_h * g_inv[:, sl]], axis=0)
        amat = _dot_nt(lhs, rhs)
        a_kb = jnp.where(strict, amat[:q, :q], 0.0)
        a_kk = jnp.where(strict, amat[:q, q:], 0.0)
        a_rb = jnp.where(incl, amat[q:, :q], 0.0)
        a_rk = jnp.where(incl, amat[q:, q:], 0.0)
        tinv = eye - a_kb
        pw = _dot(a_kb, a_kb)
        span = 2
        while span < q:
            tinv = tinv + _dot(tinv, pw)
            span *= 2
            if span < q:
                pw = _dot(pw, pw)
        s_h = s_ref[j]
        u = _dot(tinv, _dot_nt(kk * g_abse[:, sl], s_h) + _dot(a_kk, v_h))
        y = _dot_nt(r_h * g_abs[:, sl], s_h) + _dot(a_rk, v_h) - _dot(a_rb, u)
        vut = jnp.concatenate([v_h, u], axis=1).T
        s_ref[j] = (s_h * etot[:, sl] + _dot(vut[:N_B], kd_h * g_end[:, sl])
                    - _dot(vut[N_B:], bb * g_end[:, sl]))
        y_ref[0, :, sl] = y
        bonus_ref[0, :, sl] = jnp.sum(r_h * kd_h * rk_ref[:, sl], axis=-1, keepdims=True) * v_h

    @pl.when(c == nc - 1)
    def _():
        fin_ref[0, 0] = s_ref[...]


def _rwkv(proj, w0, w2, a0, a2, kkp, ka, rk, s0, nseq, seq_len):
    q = RWKV_Q
    nc = seq_len // q
    t = nseq * seq_len
    nhp = H_B // 2
    rowblk = lambda b, d, c: b * nc + c + d * (nc - 1 - 2 * c)
    vec = lambda: pl.BlockSpec((1, 128), lambda b, d, hp, c: (0, hp))
    kern = functools.partial(_rwkv_kernel, q=q)
    return pl.pallas_call(
        kern,
        grid=(nseq, 2, nhp, nc),
        in_specs=[
            pl.BlockSpec((q, 128), lambda b, d, hp, c: (rowblk(b, d, c), COL_R // 128 + hp)),
            pl.BlockSpec((q, 128), lambda b, d, hp, c: (rowblk(b, d, c), COL_K // 128 + hp)),
            pl.BlockSpec((q, 128), lambda b, d, hp, c: (rowblk(b, d, c), COL_V // 128 + hp)),
            pl.BlockSpec((q, 128), lambda b, d, hp, c: (rowblk(b, d, c), COL_LWLA // 128)),
            pl.BlockSpec((1, 1, 128), lambda b, d, hp, c: (d, 0, hp)),
            pl.BlockSpec((1, R_W, 128), lambda b, d, hp, c: (d, 0, hp)),
            pl.BlockSpec((1, 1, 128), lambda b, d, hp, c: (d, 0, hp)),
            pl.BlockSpec((1, R_A, 128), lambda b, d, hp, c: (d, 0, hp)),
            vec(), vec(), vec(),
            pl.BlockSpec((1, 1, 2, N_B, N_B), lambda b, d, hp, c: (b, d, hp, 0, 0)),
        ],
        out_specs=[
            pl.BlockSpec((1, q, 128), lambda b, d, hp, c: (d, rowblk(b, d, c), hp)),
            pl.BlockSpec((1, q, 128), lambda b, d, hp, c: (d, rowblk(b, d, c), hp)),
            pl.BlockSpec((1, 1, 2, N_B, N_B), lambda b, d, hp, c: (b, d, hp, 0, 0)),
        ],
        out_shape=[
            jax.ShapeDtypeStruct((2, t, D_B), F32),
            jax.ShapeDtypeStruct((2, t, D_B), F32),
            jax.ShapeDtypeStruct((nseq, 2, H_B, N_B, N_B), F32),
        ],
        scratch_shapes=[pltpu.VMEM((2, N_B, N_B), F32)],
        compiler_params=_cparams(),
        name="rwkv",
    )(proj, proj, proj, proj, w0.reshape(2, 1, D_B), w2, a0.reshape(2, 1, D_B), a2,
      kkp.reshape(1, D_B), ka.reshape(1, D_B), rk.reshape(1, D_B), s0)


def _s5_operators(a_re, a_im, log_dt, b_re, b_im, c_re, c_im, nsteps):
    hp = lax.Precision.HIGHEST
    q = S5_Q
    pows_re, pows_im, bf_re, bf_im, cp_re, cp_im = [], [], [], [], [], []
    for d in range(2):
        lam_re = a_re[d]
        lam_im = a_im[d]
        step = jnp.exp(log_dt[d])[:, None]
        mag = jnp.exp(lam_re * step)
        ab_re = mag * jnp.cos(lam_im * step)
        ab_im = mag * jnp.sin(lam_im * step)
        den = lam_re * lam_re + lam_im * lam_im
        f_re = ((ab_re - 1.0) * lam_re + ab_im * lam_im) / den
        f_im = (ab_im * lam_re - (ab_re - 1.0) * lam_im) / den
        bfr = f_re[..., None] * b_re[d] - f_im[..., None] * b_im[d]
        bfi = f_re[..., None] * b_im[d] + f_im[..., None] * b_re[d]
        pr = [jnp.ones_like(ab_re)]
        pi = [jnp.zeros_like(ab_re)]
        for _ in range(q):
            pr.append(pr[-1] * ab_re - pi[-1] * ab_im)
            pi.append(pr[-2] * ab_im + pi[-1] * ab_re)
        pr = jnp.stack(pr)
        pi = jnp.stack(pi)
        cpr = c_re[d][None] * pr[:, :, None, :] - c_im[d][None] * pi[:, :, None, :]
        cpi = c_re[d][None] * pi[:, :, None, :] + c_im[d][None] * pr[:, :, None, :]
        pows_re.append(pr); pows_im.append(pi); bf_re.append(bfr); bf_im.append(bfi)
        cp_re.append(cpr); cp_im.append(cpi)

    def kern(d):
        return (jnp.einsum('tgin,gnj->tgij', cp_re[d][:q], bf_re[d], precision=hp)
                - jnp.einsum('tgin,gnj->tgij', cp_im[d][:q], bf_im[d], precision=hp))

    k0, k1 = kern(0), kern(1)
    kfull = jnp.concatenate([k1[q - 1:0:-1], (k0[0] + k1[0])[None], k0[1:]], axis=0)
    s_idx = np.arange(q)[:, None]
    t_idx = np.arange(q)[None, :]
    tm = kfull[t_idx - s_idx + q - 1]
    tm = tm.transpose(2, 0, 4, 1, 3).reshape(G_C, q * C_GRP, q * C_GRP)

    def state_in(d, order):
        pr = pows_re[d][order]
        pi = pows_im[d][order]
        wr = pr[..., None] * bf_re[d][None] - pi[..., None] * bf_im[d][None]
        wi = pr[..., None] * bf_im[d][None] + pi[..., None] * bf_re[d][None]
        to = lambda w: w.transpose(1, 0, 3, 2).reshape(G_C, q * C_GRP, N_C)
        return to(wr), to(wi)

    w0r, w0i = state_in(0, np.arange(q - 1, -1, -1))
    w1r, w1i = state_in(1, np.arange(q))
    ws_re = jnp.concatenate([w0r, w1r], axis=-1)
    ws_im = jnp.concatenate([w0i, w1i], axis=-1)

    def state_out(d, order):
        to = lambda w: w[order].transpose(1, 3, 0, 2).reshape(G_C, N_C, q * C_GRP)
        return to(cp_re[d]), -to(cp_im[d])

    o0r, o0i = state_out(0, np.arange(1, q + 1))
    o1r, o1i = state_out(1, np.arange(q, 0, -1))
    wo_re = jnp.concatenate([o0r, o1r], axis=1)
    wo_im = jnp.concatenate([o0i, o1i], axis=1)

    ar = jnp.concatenate([pows_re[0][q], pows_re[1][q]], axis=-1)
    ai = jnp.concatenate([pows_im[0][q], pows_im[1][q]], axis=-1)
    aprs, apis = [ar], [ai]
    for _ in range(nsteps - 1):
        ar, ai = ar * ar - ai * ai, 2.0 * ar * ai
        aprs.append(ar); apis.append(ai)
    ap_re = jnp.stack(aprs, axis=1)
    ap_im = jnp.stack(apis, axis=1)
    return (tm.astype(BF16), ws_re.astype(BF16), ws_im.astype(BF16), wo_re.astype(BF16),
            wo_im.astype(BF16), ap_re, ap_im)


def _s5_kernel(u_ref, tm_ref, wsr_ref, wsi_ref, wor_ref, woi_ref, apr_ref, api_ref, h0r_ref, h0i_ref,
               y_ref, fr_ref, fi_ref, xr_scr, xi_scr, *, nseq, nc):
    rws = nseq * nc
    n2 = 2 * N_C
    u = u_ref[0].astype(BF16)
    er = jnp.dot(u, wsr_ref[0], preferred_element_type=F32)
    ei = jnp.dot(u, wsi_ref[0], preferred_element_type=F32)
    lane = lax.broadcasted_iota(jnp.int32, (rws, n2), 1)
    pos = lax.broadcasted_iota(jnp.int32, (rws, n2), 0) % nc
    fwd = lane < N_C
    bwd = jnp.logical_not(fwd)
    edge = jnp.logical_or(jnp.logical_and(fwd, pos == 0), jnp.logical_and(bwd, pos == nc - 1))
    h0r = jnp.broadcast_to(h0r_ref[0][:, None, :], (nseq, nc, n2)).reshape(rws, n2)
    h0i = jnp.broadcast_to(h0i_ref[0][:, None, :], (nseq, nc, n2)).reshape(rws, n2)
    aqr = apr_ref[0, 0:1, :]
    aqi = api_ref[0, 0:1, :]
    xr = er + jnp.where(edge, aqr * h0r - aqi * h0i, 0.0)
    xi = ei + jnp.where(edge, aqr * h0i + aqi * h0r, 0.0)

    def prev(x, s):
        return jnp.where(fwd, pltpu.roll(x, s, 0), pltpu.roll(x, rws - s, 0))

    s, kstep = 1, 0
    while s < nc:
        ar = apr_ref[0, kstep:kstep + 1, :]
        ai = api_ref[0, kstep:kstep + 1, :]
        pr = prev(xr, s)
        pi = prev(xi, s)
        valid = jnp.logical_or(jnp.logical_and(fwd, pos >= s), jnp.logical_and(bwd, pos + s < nc))
        xr, xi = (xr + jnp.where(valid, ar * pr - ai * pi, 0.0),
                  xi + jnp.where(valid, ar * pi + ai * pr, 0.0))
        s *= 2
        kstep += 1

    hin_r = jnp.where(edge, h0r, prev(xr, 1))
    hin_i = jnp.where(edge, h0i, prev(xi, 1))
    y_ref[0] = (jnp.dot(u, tm_ref[0], preferred_element_type=F32)
                + _dot(hin_r, wor_ref[0]) + _dot(hin_i, woi_ref[0]))

    xr_scr[...] = xr
    xi_scr[...] = xi
    lane_s = lax.broadcasted_iota(jnp.int32, (nseq, n2), 1) < N_C
    fr_ref[0] = jnp.where(lane_s, xr_scr[pl.ds(nc - 1, nseq, stride=nc), :], xr_scr[pl.ds(0, nseq, stride=nc), :])
    fi_ref[0] = jnp.where(lane_s, xi_scr[pl.ds(nc - 1, nseq, stride=nc), :], xi_scr[pl.ds(0, nseq, stride=nc), :])


def _s5(u_g, ops, h0r, h0i, nseq, nc):
    tm, wsr, wsi, wor, woi, apr, api = ops
    rws = nseq * nc
    qc = S5_Q * C_GRP
    n2 = 2 * N_C
    nsteps = apr.shape[1]
    g3 = lambda a, b: pl.BlockSpec((1, a, b), lambda g: (g, 0, 0))
    kern = functools.partial(_s5_kernel, nseq=nseq, nc=nc)
    return pl.pallas_call(
        kern,
        grid=(G_C,),
        in_specs=[g3(rws, qc), g3(qc, qc), g3(qc, n2), g3(qc, n2), g3(n2, qc), g3(n2, qc),
                  g3(nsteps, n2), g3(nsteps, n2), g3(nseq, n2), g3(nseq, n2)],
        out_specs=[g3(rws, qc), g3(nseq, n2), g3(nseq, n2)],
        out_shape=[
            jax.ShapeDtypeStruct((G_C, rws, qc), F32),
            jax.ShapeDtypeStruct((G_C, nseq, n2), F32),
            jax.ShapeDtypeStruct((G_C, nseq, n2), F32),
        ],
        scratch_shapes=[pltpu.VMEM((rws, n2), F32), pltpu.VMEM((rws, n2), F32)],
        compiler_params=_cparams(),
        name="s5",
    )(u_g, tm, wsr, wsi, wor, woi, apr, api, h0r, h0i)


def _post_kernel(ys_ref, za_ref, nrm_ref, yr_ref, bn_ref, gb_ref, lnw_ref, lnb_ref,
                 y5_ref, uc_ref, gc_ref, d5_ref, gw_ref, gbias_ref, o_ref):
    ya = (ys_ref[0] + ys_ref[1]) * _silu(za_ref[...])
    ya = ya * lax.rsqrt(jnp.mean(ya * ya, axis=-1, keepdims=True) + RMS_EPS) * nrm_ref[...]
    o_ref[0] = ya.astype(BF16)

    yb = yr_ref[0] + yr_ref[1]
    ch = lax.broadcasted_iota(jnp.int32, (D_B, H_B), 0) // N_B
    hd = lax.broadcasted_iota(jnp.int32, (D_B, H_B), 1)
    onehot = jnp.where(ch == hd, 1.0, 0.0)
    ch_t = lax.broadcasted_iota(jnp.int32, (H_B, D_B), 1) // N_B
    hd_t = lax.broadcasted_iota(jnp.int32, (H_B, D_B), 0)
    onehot_t = jnp.where(ch_t == hd_t, 1.0, 0.0)
    mu = _dot_exact_rhs(yb, onehot) * (1.0 / N_B)
    dlt = yb - _dot_exact_rhs(mu, onehot_t)
    var = _dot_exact_rhs(dlt * dlt, onehot) * (1.0 / N_B)
    rstd = _dot_exact_rhs(lax.rsqrt(var + RWKV_GN_EPS), onehot_t)
    yb = dlt * rstd * lnw_ref[...] + lnb_ref[...] + (bn_ref[0] + bn_ref[1])
    o_ref[1] = (yb * _silu(gb_ref[...])).astype(BF16)

    uc = uc_ref[...]
    yc = y5_ref[...] + d5_ref[...] * uc
    z = _dot(_gelu_tanh(yc), gw_ref[...]) + gbias_ref[...]
    yc = z[:, :D_C] * _sigmoid(z[:, D_C:])
    o_ref[2] = (yc * _silu(gc_ref[...])).astype(BF16)


def _post(ys, yr, bonus, y5, proj, ssd_norm, ln_w, ln_b, s5_d, glu_w, glu_b, tm):
    t = y5.shape[0]
    w = D_A
    row = lambda: pl.BlockSpec((1, w), lambda i: (0, 0))
    two = lambda: pl.BlockSpec((2, tm, w), lambda i: (0, i, 0))
    pcol = lambda col: pl.BlockSpec((tm, w), lambda i: (i, col // w))
    return pl.pallas_call(
        _post_kernel,
        grid=(t // tm,),
        in_specs=[two(), pcol(COL_ZA), row(), two(), two(), pcol(COL_GB), row(), row(),
                  pl.BlockSpec((tm, w), lambda i: (i, 0)), pcol(COL_UC), pcol(COL_GC), row(),
                  pl.BlockSpec((D_C, 2 * D_C), lambda i: (0, 0)),
                  pl.BlockSpec((1, 2 * D_C), lambda i: (0, 0))],
        out_specs=pl.BlockSpec((3, tm, w), lambda i: (0, i, 0)),
        out_shape=jax.ShapeDtypeStruct((3, t, w), BF16),
        compiler_params=_cparams(),
        name="post",
    )(ys, proj, ssd_norm.reshape(1, w), yr, bonus, proj, ln_w.reshape(1, w), ln_b.reshape(1, w),
      y5, proj, proj, s5_d.reshape(1, w), glu_w, glu_b.reshape(1, 2 * D_C))


def _mix_kernel(y_ref, gl_ref, wb_ref, wmo_ref, h_ref, gate_ref, gpost_ref, o_ref, acc_ref):
    k = pl.program_id(1)
    contrib = _sigmoid(gl_ref[...]) * jnp.dot(y_ref[0], wb_ref[0], preferred_element_type=F32)

    @pl.when(k == 0)
    def _():
        acc_ref[...] = contrib

    @pl.when(k > 0)
    def _():
        acc_ref[...] += contrib

    @pl.when(k == N_BRANCH - 1)
    def _():
        out = jnp.dot(acc_ref[...].astype(BF16), wmo_ref[...], preferred_element_type=F32)
        nrm = out * lax.rsqrt(jnp.mean(out * out, axis=-1, keepdims=True) + RMS_EPS) * gpost_ref[...]
        o_ref[...] = h_ref[...] + gate_ref[0] * nrm


def _mix(y3, proj, h, modr, wb, wmo, g_post, row_base, blocks_per_row, tm):
    t, d = h.shape
    return pl.pallas_call(
        _mix_kernel,
        grid=(t // tm, N_BRANCH),
        in_specs=[
            pl.BlockSpec((1, tm, D_A), lambda i, k: (k, i, 0)),
            pl.BlockSpec((tm, d), lambda i, k: (i, COL_GL // d + k)),
            pl.BlockSpec((1, D_A, d), lambda i, k: (k, 0, 0)),
            pl.BlockSpec((d, d), lambda i, k: (0, 0)),
            pl.BlockSpec((tm, d), lambda i, k: (i, 0)),
            pl.BlockSpec((1, 1, d), lambda i, k: ((row_base + i // blocks_per_row) * 3 + 2, 0, 0)),
            pl.BlockSpec((1, d), lambda i, k: (0, 0)),
        ],
        out_specs=pl.BlockSpec((tm, d), lambda i, k: (i, 0)),
        out_shape=jax.ShapeDtypeStruct((t, d), F32),
        scratch_shapes=[pltpu.VMEM((tm, d), F32)],
        compiler_params=_cparams(),
        name="mix",
    )(y3, proj, wb, wmo, h, modr, g_post.reshape(1, d))


def _apply_layer(h, modr, lp, st0, nseq, seq_len, line_len, row_base):
    t = nseq * seq_len
    shared_mod = row_base != 0
    tm_proj = 1024 if shared_mod else min(1024, seq_len)
    proj = _proj(h, modr, lp['g_pre'], lp['w_in_p'], row_base,
                 t // tm_proj if shared_mod else seq_len // tm_proj, tm_proj)

    s_ssd, s_rwkv, s_re, s_im = st0
    dtat = proj[:, COL_DTA:COL_DTA + H_A].T
    h0t = s_ssd.reshape(nseq, 2, D_A, N_A).transpose(0, 1, 3, 2)
    ys, fin_ssd = _ssd(proj, dtat, lp['conv_w'], lp['conv_b'], lp['ssd_dt_bias'], lp['ssd_a_log'],
                       lp['ssd_d'], h0t, nseq, seq_len, line_len)
    fin_ssd = fin_ssd.transpose(0, 1, 3, 2).reshape(nseq, 2, H_A, P_A, N_A)

    yr, bonus, fin_rwkv = _rwkv(proj, lp['rw_w0'], lp['rw_w2'], lp['rw_a0'], lp['rw_a2'], lp['rw_kk'],
                                lp['rw_ka'], lp['rw_rk'], s_rwkv, nseq, seq_len)

    nc5 = seq_len // S5_Q
    uc = proj[:, COL_UC:COL_UC + D_C]
    u_g = uc.reshape(nseq * nc5, S5_Q, G_C, C_GRP).transpose(2, 0, 1, 3).reshape(G_C, nseq * nc5, S5_Q * C_GRP)
    to_g = lambda s: s.transpose(2, 0, 1, 3).reshape(G_C, nseq, 2 * N_C)
    y5_g, fr, fi = _s5(u_g, lp['s5_ops'], to_g(s_re), to_g(s_im), nseq, nc5)
    y5 = y5_g.reshape(G_C, nseq * nc5, S5_Q, C_GRP).transpose(1, 2, 0, 3).reshape(t, D_C)
    from_g = lambda f: f.reshape(G_C, nseq, 2, N_C).transpose(1, 2, 0, 3)

    y3 = _post(ys, yr, bonus, y5, proj, lp['ssd_norm'], lp['rw_ln_w'], lp['rw_ln_b'], lp['s5_d'],
               lp['s5_glu_w'], lp['s5_glu_b'], 256)
    tm_mix = 256
    hn = _mix(y3, proj, h, modr, lp['w_branch'], lp['w_mo'], lp['g_post'], row_base,
              t // tm_mix if shared_mod else seq_len // tm_mix, tm_mix)
    return hn, (fin_ssd, fin_rwkv, from_g(fr), from_g(fi))


def _reorder_w_in(w_in):
    dpt, d, _ = w_in.shape
    z = lambda n: jnp.zeros((dpt, d, n), w_in.dtype)
    cols = [
        w_in[:, :, 0:2048],
        w_in[:, :, 9360:15504],
        w_in[:, :, 2048:3072],
        w_in[:, :, 3088:7184],
        w_in[:, :, 7312:9360],
        w_in[:, :, 3072:3088],
        z(COL_LWLA - COL_DTA - H_A),
        w_in[:, :, 7184:7312],
        z(N_PROJ - COL_LWLA - 128),
    ]
    return jnp.concatenate(cols, axis=-1).astype(BF16)


def kernel(x_prompt, x_sample, c, state_ssd, state_rwkv, state_s5_re, state_s5_im, c_ctx, w_ada, b_ada, g_pre, g_post, w_in, conv_w, conv_b, ssd_a_log, ssd_dt_bias, ssd_d, ssd_norm, rw_w0, rw_w2, rw_a0, rw_a2, rw_kk, rw_ka, rw_rk, rw_ln_w, rw_ln_b, s5_a_re, s5_a_im, s5_log_dt, s5_b_re, s5_b_im, s5_c_re, s5_c_im, s5_d, s5_glu_w, s5_glu_b, w_branch, w_mo):
    bp, lp_len, d = x_prompt.shape
    bs, ls_len, _ = x_sample.shape
    depth = w_in.shape[0]
    rows = ls_len // GRID_W

    cond8 = jnp.concatenate([c, c_ctx[None, :], jnp.zeros((8 - bs - 1, d), F32)], axis=0)
    mod = _adaln(cond8, w_ada, b_ada)
    w_in_p = _reorder_w_in(w_in)
    wb16 = w_branch.astype(BF16)
    wmo16 = w_mo.astype(BF16)
    glu16 = s5_glu_w.astype(BF16)
    nsteps = max(int(math.log2(ls_len // S5_Q)), int(math.log2(lp_len // S5_Q)))

    def layer_params(l):
        return {
            'g_pre': g_pre[l], 'g_post': g_post[l], 'w_in_p': w_in_p[l],
            'conv_w': conv_w[l], 'conv_b': conv_b[l], 'ssd_a_log': ssd_a_log[l],
            'ssd_dt_bias': ssd_dt_bias[l], 'ssd_d': ssd_d[l], 'ssd_norm': ssd_norm[l],
            'rw_w0': rw_w0[l], 'rw_w2': rw_w2[l], 'rw_a0': rw_a0[l], 'rw_a2': rw_a2[l],
            'rw_kk': rw_kk[l], 'rw_ka': rw_ka[l], 'rw_rk': rw_rk[l],
            'rw_ln_w': rw_ln_w[l], 'rw_ln_b': rw_ln_b[l],
            's5_ops': _s5_operators(s5_a_re[l], s5_a_im[l], s5_log_dt[l], s5_b_re[l], s5_b_im[l],
                                    s5_c_re[l], s5_c_im[l], nsteps),
            's5_d': s5_d[l], 's5_glu_w': glu16[l], 's5_glu_b': s5_glu_b[l],
            'w_branch': wb16[l], 'w_mo': wmo16[l],
        }

    params = [layer_params(l) for l in range(depth)]
    modr = mod.reshape(depth, 8 * 3, 1, d)

    zero_state = (jnp.zeros((bp, 2, H_A, P_A, N_A), F32), jnp.zeros((bp, 2, H_B, N_B, N_B), F32),
                  jnp.zeros((bp, 2, G_C, N_C), F32), jnp.zeros((bp, 2, G_C, N_C), F32))
    h = x_prompt.reshape(bp * lp_len, d)
    new_ssd, new_rwkv, new_re, new_im = [], [], [], []
    for l in range(depth):
        h, (sa, sb, sre, sim) = _apply_layer(h, modr[l], params[l], zero_state, bp, lp_len, lp_len, bs)
        new_ssd.append(sa)
        new_rwkv.append(sb)
        new_re.append(sre)
        new_im.append(sim)

    def raster_t(x, a, b):
        return x.reshape(bs, a, b, d).transpose(0, 2, 1, 3).reshape(bs * ls_len, d)

    hs = x_sample.reshape(bs * ls_len, d)
    for l in range(depth):
        st0 = (state_ssd[:, l], state_rwkv[:, l], state_s5_re[:, l], state_s5_im[:, l])
        if l % 2 == 1:
            hs = raster_t(hs, rows, GRID_W)
            hs, _ = _apply_layer(hs, modr[l], params[l], st0, bs, ls_len, rows, 0)
            hs = raster_t(hs, GRID_W, rows)
        else:
            hs, _ = _apply_layer(hs, modr[l], params[l], st0, bs, ls_len, GRID_W, 0)

    return (h.reshape(bp, lp_len, d), hs.reshape(bs, ls_len, d), jnp.stack(new_ssd, axis=1),
            jnp.stack(new_rwkv, axis=1), jnp.stack(new_re, axis=1), jnp.stack(new_im, axis=1))
```
